```python
import jax, jax.numpy as jnp
from jax import lax
import numpy as np

D_MODEL = 1024
BATCH = 1
SEQ = 16384
DEPTH = 1
DEC_BATCH = 128
DEC_SEQ = 8
PAST_LEN = 8192
PAGE_SIZE = 128

D_LRU = ((4 * D_MODEL // 3) // 64) * 64
N_LRU_BLOCKS = 16
CONV_W = 4
RG_C = 8.0
DILATED_GROUPS = ((128, 1), (512, 4), (2048, 16))
N_GROUPS = 3
HEADS_PER_GROUP = 8
HEAD_DIM = 64
ATT_HEADS = N_GROUPS * HEADS_PER_GROUP
ATT_WIDTH = ATT_HEADS * HEAD_DIM
ATT_OUT_WIDTH = HEADS_PER_GROUP * HEAD_DIM
ROPE_THETA = 10000.0
PEER_HEADS = 8
PEER_N_KEYS = 128
PEER_N_EXPERTS = PEER_N_KEYS * PEER_N_KEYS
PEER_TOPK = 16
PEER_D_KEY = 256
PEER_BLOCK = 256
IN_WIDTH = 2 * D_LRU + 3 * ATT_WIDTH + 2 * D_MODEL
NORM_EPS = 1e-6
NEG_INF = -1e30

kernel_name = "hawk_dilated_peer_decoder_step"

F32 = jnp.float32


def rms_norm(x, gain):
    xf = x.astype(F32)
    var = jnp.mean(xf * xf, axis=-1, keepdims=True)
    return (xf * lax.rsqrt(var + NORM_EPS) * gain.astype(F32)).astype(x.dtype)


def rope(x, pos):
    half = HEAD_DIM // 2
    inv = ROPE_THETA ** (-jnp.arange(half, dtype=F32) / half)
    ang = pos.astype(F32)[:, None] * inv[None, :]
    bshape = (1, ang.shape[0]) + (1,) * (x.ndim - 3) + (half,)
    cos = jnp.cos(ang).reshape(bshape)
    sin = jnp.sin(ang).reshape(bshape)
    x1 = x[..., :half].astype(F32)
    x2 = x[..., half:].astype(F32)
    return jnp.concatenate([x1 * cos - x2 * sin, x2 * cos + x1 * sin], axis=-1).astype(x.dtype)


def lin_combine(left, right):
    a1, b1 = left
    a2, b2 = right
    return a1 * a2, a2 * b1 + b2


def dilated_band_attention(q, k, v, window, dil):
    B, T, H, hd = q.shape
    blk = window // dil
    span = dil * blk
    tp = -(-T // span) * span
    s_len = tp // dil
    nb = s_len // blk

    def to_res(a):
        a = jnp.pad(a, ((0, 0), (0, tp - T), (0, 0), (0, 0)))
        return a.reshape(B, s_len, dil, H, hd).transpose(0, 2, 1, 3, 4).reshape(B, dil, nb, blk, H, hd)

    def band(a):
        prev = jnp.pad(a[:, :, :-1], ((0, 0), (0, 0), (1, 0), (0, 0), (0, 0), (0, 0)))
        return jnp.concatenate([prev, a], axis=3)

    qr = to_res(q)
    kb = band(to_res(k))
    vb = band(to_res(v))
    s = jnp.einsum('brnqhc,brnkhc->brnhqk', qr, kb, preferred_element_type=F32) * (HEAD_DIM ** -0.5)
    qi = jnp.arange(blk)[:, None]
    kj = jnp.arange(2 * blk)[None, :]
    ni = jnp.arange(nb)[:, None, None]
    dist = qi + blk - kj
    valid = (dist >= 0) & (dist <= blk) & (ni * blk + kj - blk >= 0)
    s = jnp.where(valid[None, None, :, None], s, NEG_INF)
    m = jnp.max(s, axis=-1, keepdims=True)
    p = jnp.exp(s - m)
    den = jnp.sum(p, axis=-1, keepdims=True)
    o = jnp.einsum('brnhqk,brnkhc->brnqhc', p, vb.astype(F32)) / jnp.swapaxes(den, 3, 4)
    lse = jnp.swapaxes((m + jnp.log(den))[..., 0], 3, 4)

    def from_res(a):
        tail = a.shape[4:]
        a = a.reshape((B, dil, s_len) + tail).swapaxes(1, 2).reshape((B, tp) + tail)
        return a[:, :T]

    return from_res(o), from_res(lse)


def dilated_cached_attention(q, k, v, kv_buf, window, dil):
    DS = q.shape[1]
    wb = kv_buf.shape[1]
    n_keys = window // dil
    kv_ext = jnp.concatenate([kv_buf, jnp.stack([k, v], axis=2).astype(kv_buf.dtype)], axis=1)
    idx = wb + jnp.arange(DS)[:, None] - dil * jnp.arange(n_keys + 1)[None, :]
    valid = idx >= 0
    kv_sel = kv_ext[:, jnp.maximum(idx, 0)]
    s = jnp.einsum('bqhc,bqkhc->bhqk', q, kv_sel[:, :, :, 0], preferred_element_type=F32) * (HEAD_DIM ** -0.5)
    s = jnp.where(valid[None, None], s, NEG_INF)
    m = jnp.max(s, axis=-1, keepdims=True)
    p = jnp.exp(s - m)
    den = jnp.sum(p, axis=-1, keepdims=True)
    o = jnp.einsum('bhqk,bqkhc->bqhc', p, kv_sel[:, :, :, 1].astype(F32)) / jnp.transpose(den, (0, 2, 1, 3))
    lse = jnp.transpose((m + jnp.log(den))[..., 0], (0, 2, 1))
    return o, lse, kv_ext[:, DS:]


def temporal_block(hn, pos, kv_bufs, h0, conv_buf, w_in, conv_w, conv_b, w_rg, b_rg, w_ig, b_ig,
                   lru_lambda, w_lru_out, w_att_out, w_out):
    B, T, _ = hn.shape
    dt = hn.dtype
    z = jnp.einsum('btd,de->bte', hn, w_in)
    cuts = [int(c) for c in np.cumsum([D_LRU, D_LRU, ATT_WIDTH, ATT_WIDTH, ATT_WIDTH, D_MODEL])]
    xl, yl, q, k, v, g_lru, g_att = jnp.split(z, cuts, axis=-1)

    ext = jnp.concatenate([conv_buf.astype(dt), xl], axis=1)
    xc = conv_b
    for i in range(CONV_W):
        xc = xc + conv_w[i] * ext[:, i:i + T]
    new_conv = ext[:, T:]
    xb = xc.reshape(B, T, N_LRU_BLOCKS, D_LRU // N_LRU_BLOCKS)
    r = jax.nn.sigmoid((jnp.einsum('btnc,ncd->btnd', xb, w_rg).reshape(B, T, D_LRU) + b_rg).astype(F32))
    ig = jax.nn.sigmoid((jnp.einsum('btnc,ncd->btnd', xb, w_ig).reshape(B, T, D_LRU) + b_ig).astype(F32))
    log_a = -RG_C * r * jax.nn.softplus(-lru_lambda.astype(F32))
    a = jnp.exp(log_a)
    u = jnp.sqrt(-jnp.expm1(2.0 * log_a)) * ig * xc.astype(F32)
    u = u.at[:, 0].add(a[:, 0] * h0.astype(F32))
    _, hs = lax.associative_scan(lin_combine, (a, u), axis=1)
    lru_y = (hs * jax.nn.gelu(yl.astype(F32))).astype(dt)
    lru_proj = jnp.einsum('bte,ed->btd', lru_y, w_lru_out)

    gshape = (B, T, N_GROUPS, HEADS_PER_GROUP, HEAD_DIM)
    q = rope(q.reshape(gshape), pos)
    k = rope(k.reshape(gshape), pos)
    v = v.reshape(gshape)
    outs, lses, new_kv = [], [], []
    for g, (window, dil) in enumerate(DILATED_GROUPS):
        if kv_bufs is None:
            o, lse = dilated_band_attention(q[:, :, g], k[:, :, g], v[:, :, g], window, dil)
            wb = min(window, T)
            nkv = jnp.stack([k[:, T - wb:, g], v[:, T - wb:, g]], axis=2)
        else:
            o, lse, nkv = dilated_cached_attention(q[:, :, g], k[:, :, g], v[:, :, g], kv_bufs[g], window, dil)
        outs.append(o)
        lses.append(lse)
        new_kv.append(nkv)
    alpha = jax.nn.softmax(jnp.stack(lses, axis=0), axis=0)
    o = jnp.einsum('gbth,gbthc->bthc', alpha, jnp.stack(outs, axis=0)).reshape(B, T, ATT_OUT_WIDTH).astype(dt)
    att_proj = jnp.einsum('bte,ed->btd', o, w_att_out)

    merged = jax.nn.sigmoid(g_lru) * lru_proj + jax.nn.sigmoid(g_att) * att_proj
    out = jnp.einsum('btd,de->bte', merged, w_out)
    return out, new_kv, hs[:, -1].astype(dt), new_conv


def peer_ffn(hn, w_peer_q, peer_sub_keys, peer_u, peer_v):
    B, T, D = hn.shape
    n_tok = B * T
    n_blk = -(-n_tok // PEER_BLOCK)
    xt = jnp.pad(hn.reshape(n_tok, D), ((0, n_blk * PEER_BLOCK - n_tok), (0, 0))).reshape(n_blk, PEER_BLOCK, D)
    half = PEER_D_KEY // 2

    def block(xb):
        qh = jnp.einsum('nd,dhk->nhk', xb, w_peer_q).astype(F32)
        s1 = jnp.einsum('nhc,hkc->nhk', qh[..., :half], peer_sub_keys[0].astype(F32))
        s2 = jnp.einsum('nhc,hkc->nhk', qh[..., half:], peer_sub_keys[1].astype(F32))
        v1, i1 = lax.top_k(s1, PEER_TOPK)
        v2, i2 = lax.top_k(s2, PEER_TOPK)
        cand = (v1[..., :, None] + v2[..., None, :]).reshape(xb.shape[0], PEER_HEADS, -1)
        cidx = (i1[..., :, None] * PEER_N_KEYS + i2[..., None, :]).reshape(xb.shape[0], PEER_HEADS, -1)
        best, sel = lax.top_k(cand, PEER_TOPK)
        eidx = jnp.take_along_axis(cidx, sel, axis=-1)
        gate = jax.nn.softmax(best, axis=-1)
        u = jnp.take(peer_u, eidx, axis=0)
        act = jax.nn.gelu(jnp.einsum('nd,nhkd->nhk', xb, u).astype(F32))
        vv = jnp.take(peer_v, eidx, axis=0)
        return jnp.einsum('nhk,nhkd->nd', (gate * act).astype(xb.dtype), vv)

    out = lax.map(block, xt)
    return out.reshape(-1, D)[:n_tok].reshape(B, T, D)


def trunk(x, pos, kv_caches, lru_h, conv_state, norm_mix, w_in, conv_w, conv_b, w_rg, b_rg, w_ig, b_ig,
          lru_lambda, w_lru_out, w_att_out, w_out, norm_ffn, w_peer_q, peer_sub_keys, peer_u, peer_v, norm_final):
    new_kv = [[] for _ in DILATED_GROUPS]
    new_h, new_conv = [], []
    for l in range(DEPTH):
        bufs = None if kv_caches is None else [c[l] for c in kv_caches]
        hn = rms_norm(x, norm_mix[l])
        mix, kvs, h_last, conv_last = temporal_block(
            hn, pos, bufs, lru_h[l], conv_state[l], w_in[l], conv_w[l], conv_b[l], w_rg[l], b_rg[l],
            w_ig[l], b_ig[l], lru_lambda[l], w_lru_out[l], w_att_out[l], w_out[l])
        x = x + mix
        x = x + peer_ffn(rms_norm(x, norm_ffn[l]), w_peer_q[l], peer_sub_keys[l], peer_u[l], peer_v[l])
        for g in range(N_GROUPS):
            new_kv[g].append(kvs[g])
        new_h.append(h_last)
        new_conv.append(conv_last)
    y = rms_norm(x, norm_final)
    return (y, jnp.stack(new_kv[0]), jnp.stack(new_kv[1]), jnp.stack(new_kv[2]),
            jnp.stack(new_h), jnp.stack(new_conv))


def setup_inputs(seed: int = 0) -> dict:
    key = jax.random.key(seed)
    ks = jax.random.split(key, 32)
    nrm = jax.random.normal
    bw = D_LRU // N_LRU_BLOCKS
    a0 = jax.random.uniform(ks[10], (DEPTH, D_LRU), minval=0.9, maxval=0.999)
    p = a0 ** (1.0 / RG_C)
    lam = jnp.log(p) - jnp.log1p(-p)
    cshape = lambda w: (DEPTH, DEC_BATCH, min(w, PAST_LEN), 2, HEADS_PER_GROUP, HEAD_DIM)
    return {
        "x_prompt": nrm(ks[0], (BATCH, SEQ, D_MODEL), F32),
        "x_sample": nrm(ks[1], (DEC_BATCH, DEC_SEQ, D_MODEL), F32),
        "cache_kv_g1": nrm(ks[2], cshape(DILATED_GROUPS[0][0]), F32),
        "cache_kv_g2": nrm(ks[3], cshape(DILATED_GROUPS[1][0]), F32),
        "cache_kv_g3": nrm(ks[4], cshape(DILATED_GROUPS[2][0]), F32),
        "state_lru_h": 0.5 * nrm(ks[5], (DEPTH, DEC_BATCH, D_LRU), F32),
        "state_conv": 0.5 * nrm(ks[6], (DEPTH, DEC_BATCH, CONV_W - 1, D_LRU), F32),
        "norm_mix": 1.0 + 0.01 * nrm(ks[7], (DEPTH, D_MODEL), F32),
        "w_in": nrm(ks[8], (DEPTH, D_MODEL, IN_WIDTH), F32) * D_MODEL ** -0.5,
        "conv_w": nrm(ks[9], (DEPTH, CONV_W, D_LRU), F32) * CONV_W ** -0.5,
        "conv_b": 0.01 * nrm(ks[11], (DEPTH, D_LRU), F32),
        "w_rg": nrm(ks[12], (DEPTH, N_LRU_BLOCKS, bw, bw), F32) * bw ** -0.5,
        "b_rg": 0.01 * nrm(ks[13], (DEPTH, D_LRU), F32),
        "w_ig": nrm(ks[14], (DEPTH, N_LRU_BLOCKS, bw, bw), F32) * bw ** -0.5,
        "b_ig": 0.01 * nrm(ks[15], (DEPTH, D_LRU), F32),
        "lru_lambda": lam,
        "w_lru_out": nrm(ks[16], (DEPTH, D_LRU, D_MODEL), F32) * D_LRU ** -0.5,
        "w_att_out": nrm(ks[17], (DEPTH, ATT_OUT_WIDTH, D_MODEL), F32) * ATT_OUT_WIDTH ** -0.5,
        "w_out": nrm(ks[18], (DEPTH, D_MODEL, D_MODEL), F32) * D_MODEL ** -0.5,
        "norm_ffn": 1.0 + 0.01 * nrm(ks[19], (DEPTH, D_MODEL), F32),
        "w_peer_q": nrm(ks[20], (DEPTH, D_MODEL, PEER_HEADS, PEER_D_KEY), F32) * D_MODEL ** -0.5,
        "peer_sub_keys": nrm(ks[21], (DEPTH, 2, PEER_HEADS, PEER_N_KEYS, PEER_D_KEY // 2), F32) * (PEER_D_KEY // 2) ** -0.5,
        "peer_u": nrm(ks[22], (DEPTH, PEER_N_EXPERTS, D_MODEL), F32) * D_MODEL ** -0.5,
        "peer_v": nrm(ks[23], (DEPTH, PEER_N_EXPERTS, D_MODEL), F32) * PEER_HEADS ** -0.5,
        "norm_final": 1.0 + 0.01 * nrm(ks[24], (D_MODEL,), F32),
    }


def reference(x_prompt, x_sample, cache_kv_g1, cache_kv_g2, cache_kv_g3, state_lru_h, state_conv,
              norm_mix, w_in, conv_w, conv_b, w_rg, b_rg, w_ig, b_ig, lru_lambda, w_lru_out, w_att_out,
              w_out, norm_ffn, w_peer_q, peer_sub_keys, peer_u, peer_v, norm_final):
    B, T, _ = x_prompt.shape
    DB, DS, _ = x_sample.shape
    pos_p = jnp.arange(T, dtype=jnp.int32)
    h0_p = jnp.zeros((DEPTH, B, D_LRU), x_prompt.dtype)
    conv0_p = jnp.zeros((DEPTH, B, CONV_W - 1, D_LRU), x_prompt.dtype)
    y_prompt, kv1_p, kv2_p, kv3_p, h_p, conv_p = trunk(
        x_prompt, pos_p, None, h0_p, conv0_p, norm_mix, w_in, conv_w, conv_b, w_rg, b_rg, w_ig, b_ig,
        lru_lambda, w_lru_out, w_att_out, w_out, norm_ffn, w_peer_q, peer_sub_keys, peer_u, peer_v, norm_final)
    pos_s = PAST_LEN + jnp.arange(DS, dtype=jnp.int32)
    y_sample, kv1_s, kv2_s, kv3_s, h_s, conv_s = trunk(
        x_sample, pos_s, (cache_kv_g1, cache_kv_g2, cache_kv_g3), state_lru_h, state_conv,
        norm_mix, w_in, conv_w, conv_b, w_rg, b_rg, w_ig, b_ig, lru_lambda, w_lru_out, w_att_out, w_out,
        norm_ffn, w_peer_q, peer_sub_keys, peer_u, peer_v, norm_final)
    return (y_prompt, y_sample, kv1_p, kv2_p, kv3_p, h_p, conv_p, kv1_s, kv2_s, kv3_s, h_s, conv_s)
```

```python
import functools

import jax
import jax.numpy as jnp
import numpy as np
from jax import lax
from jax.experimental import pallas as pl
from jax.experimental.pallas import tpu as pltpu

F32 = jnp.float32
BF16 = jnp.bfloat16

D_MODEL = 1024
D_LRU = 1344
D_LRU_PAD = 1408
N_LRU_BLOCKS = 16
CONV_W = 4
RG_C = 8.0
DILATED_GROUPS = ((128, 1), (512, 4), (2048, 16))
N_GROUPS = 3
HEADS = 8
HEAD_DIM = 64
GROUP_WIDTH = HEADS * HEAD_DIM
ATT_WIDTH = N_GROUPS * GROUP_WIDTH
ATT_BLK = 128
ROPE_THETA = 10000.0
PEER_HEADS = 8
PEER_N_KEYS = 128
PEER_TOPK = 16
PEER_D_KEY = 256
NORM_EPS = 1e-6
NEG_INF = -1e30
NEG_BIG = -3.0e38

VMEM_LIMIT = 56 * 1024 * 1024


def _cparams(n_axes):
    return pltpu.CompilerParams(
        dimension_semantics=("arbitrary",) * n_axes, vmem_limit_bytes=VMEM_LIMIT)


def _full(shape):
    nd = len(shape)
    return pl.BlockSpec(shape, lambda *_: (0,) * nd)


def _rope_rows(z, cos_t, sin_t):
    rows, width = z.shape
    reps = width // 128
    cos_w = jnp.concatenate([cos_t] * reps, axis=1)
    sin_w = jnp.concatenate([sin_t] * reps, axis=1)
    lane = lax.broadcasted_iota(jnp.int32, (rows, width), 1)
    first_half = (lane & 63) < 32
    swapped = jnp.where(first_half,
                        pltpu.roll(z, width - 32, axis=1),
                        pltpu.roll(z, 32, axis=1))
    return z * cos_w + swapped * sin_w


def _inproj_kernel(*refs, widths, rope):
    x_ref, g_ref, w_ref = refs[:3]
    if any(rope):
        cos_ref, sin_ref = refs[3:5]
        out_refs = refs[5:]
    else:
        out_refs = refs[3:]
    x = x_ref[...]
    var = jnp.mean(x * x, axis=-1, keepdims=True)
    hn = (x * lax.rsqrt(var + NORM_EPS) * g_ref[...]).astype(BF16)
    off = 0
    for o_ref, w, rp in zip(out_refs, widths, rope):
        z = jnp.dot(hn, w_ref[:, off:off + w], preferred_element_type=F32)
        if rp:
            z = _rope_rows(z, cos_ref[...], sin_ref[...])
        o_ref[...] = z
        off += w


def _inproj(x, gain, w, widths, rope, cos_t=None, sin_t=None, tm=512):
    n = x.shape[0]
    wt = sum(widths)
    in_specs = [pl.BlockSpec((tm, D_MODEL), lambda i: (i, 0)),
                _full((1, D_MODEL)),
                _full((D_MODEL, wt))]
    args = [x, gain, w]
    if any(rope):
        in_specs += [pl.BlockSpec((tm, 128), lambda i: (i, 0))] * 2
        args += [cos_t, sin_t]
    return pl.pallas_call(
        functools.partial(_inproj_kernel, widths=widths, rope=rope),
        grid=(n // tm,),
        in_specs=in_specs,
        out_specs=[pl.BlockSpec((tm, wd), lambda i: (i, 0)) for wd in widths],
        out_shape=[jax.ShapeDtypeStruct((n, wd), F32) for wd in widths],
        compiler_params=_cparams(1),
        name="inproj",
    )(*args)


def _softplus(z):
    return jnp.maximum(z, 0.0) + jnp.log1p(jnp.exp(-jnp.abs(z)))


def _lru_gates(xc, wg_ref, bg_ref, lam_ref):
    g = jnp.dot(xc.astype(BF16), wg_ref[...], preferred_element_type=F32) + bg_ref[...]
    r = jax.nn.sigmoid(g[:, :D_LRU_PAD])
    ig = jax.nn.sigmoid(g[:, D_LRU_PAD:])
    log_a = (-RG_C * r) * _softplus(-lam_ref[...])
    a = jnp.exp(log_a)
    u = jnp.sqrt(1.0 - jnp.exp(2.0 * log_a)) * ig * xc
    return a, u


def _scan_rows(a, u, row, length):
    s = 1
    while s < length:
        a_sh = pltpu.roll(a, s, axis=0)
        u_sh = pltpu.roll(u, s, axis=0)
        m = row >= s
        u = jnp.where(m, a * u_sh + u, u)
        a = jnp.where(m, a * a_sh, a)
        s *= 2
    return a, u


def _lru_prompt_kernel(xl_ref, yl_ref, cw_ref, cb_ref, wg_ref, bg_ref, lam_ref,
                       y_ref, hlast_ref, carry_x, carry_h, *, rows):
    @pl.when(pl.program_id(0) == 0)
    def _():
        carry_x[...] = jnp.zeros_like(carry_x)
        carry_h[...] = jnp.zeros_like(carry_h)

    xl = xl_ref[...]
    ext = jnp.concatenate([carry_x[...], xl], axis=0)
    xc = cb_ref[...]
    for i in range(CONV_W):
        s = CONV_W - 1 - i
        tap = xl if s == 0 else pltpu.roll(ext, s, axis=0)[8:]
        xc = xc + cw_ref[i:i + 1, :] * tap
    a, u = _lru_gates(xc, wg_ref, bg_ref, lam_ref)
    row = lax.broadcasted_iota(jnp.int32, a.shape, 0)
    a_cum, u_cum = _scan_rows(a, u, row, rows)
    h = a_cum * carry_h[0:1, :] + u_cum
    y_ref[...] = (h * jax.nn.gelu(yl_ref[...])).astype(y_ref.dtype)
    h_last = jnp.broadcast_to(h[rows - 1:rows, :], (8, D_LRU_PAD))
    carry_h[...] = h_last
    hlast_ref[...] = h_last
    carry_x[...] = xl[rows - 8:, :]


def _lru_sample_kernel(xl_ref, yl_ref, st_ref, h0_ref, cw_ref, cb_ref, wg_ref, bg_ref, lam_ref,
                       y_ref, hs_ref, *, rows, seq):
    xl = xl_ref[...]
    st = st_ref[...]
    row = lax.broadcasted_iota(jnp.int32, xl.shape, 0) & (seq - 1)
    xc = cb_ref[...]
    for i in range(CONV_W):
        s = CONV_W - 1 - i
        if s == 0:
            tap = xl
        else:
            tap = jnp.where(row >= s, pltpu.roll(xl, s, axis=0),
                            pltpu.roll(st, rows - (seq - s), axis=0))
        xc = xc + cw_ref[i:i + 1, :] * tap
    a, u = _lru_gates(xc, wg_ref, bg_ref, lam_ref)
    u = u + jnp.where(row == 0, a * h0_ref[...], 0.0)
    _, h = _scan_rows(a, u, row, seq)
    y_ref[...] = (h * jax.nn.gelu(yl_ref[...])).astype(y_ref.dtype)
    hs_ref[...] = h


def _lru_weight_specs():
    return [_full((8, D_LRU_PAD)), _full((1, D_LRU_PAD)),
            _full((D_LRU_PAD, 2 * D_LRU_PAD)), _full((1, 2 * D_LRU_PAD)), _full((1, D_LRU_PAD))]


def _lru_prompt(xl, yl, lw, rows=256):
    n = xl.shape[0]
    tile = pl.BlockSpec((rows, D_LRU_PAD), lambda i: (i, 0))
    return pl.pallas_call(
        functools.partial(_lru_prompt_kernel, rows=rows),
        grid=(n // rows,),
        in_specs=[tile, tile] + _lru_weight_specs(),
        out_specs=[tile, _full((8, D_LRU_PAD))],
        out_shape=[jax.ShapeDtypeStruct((n, D_LRU_PAD), BF16),
                   jax.ShapeDtypeStruct((8, D_LRU_PAD), F32)],
        scratch_shapes=[pltpu.VMEM((8, D_LRU_PAD), F32), pltpu.VMEM((8, D_LRU_PAD), F32)],
        compiler_params=_cparams(1),
        name="lru_prompt",
    )(xl, yl, *lw)


def _lru_sample(xl, yl, st, h0, lw, seq, rows=256):
    n = xl.shape[0]
    tile = pl.BlockSpec((rows, D_LRU_PAD), lambda i: (i, 0))
    return pl.pallas_call(
        functools.partial(_lru_sample_kernel, rows=rows, seq=seq),
        grid=(n // rows,),
        in_specs=[tile, tile, tile, tile] + _lru_weight_specs(),
        out_specs=[tile, tile],
        out_shape=[jax.ShapeDtypeStruct((n, D_LRU_PAD), BF16),
                   jax.ShapeDtypeStruct((n, D_LRU_PAD), F32)],
        compiler_params=_cparams(1),
        name="lru_sample",
    )(xl, yl, st, h0, *lw)


def _band_attn_kernel(q_ref, kp_ref, kc_ref, vp_ref, vc_ref, o_ref, l_ref):
    mb = pl.program_id(1)
    blk = ATT_BLK
    q = q_ref[...] * (HEAD_DIM ** -0.5)
    kcat = jnp.concatenate([kp_ref[...], kc_ref[...]], axis=0)
    vcat = jnp.concatenate([vp_ref[...], vc_ref[...]], axis=0)
    qi = lax.broadcasted_iota(jnp.int32, (2 * blk, 2 * blk), 0) & (blk - 1)
    kj = lax.broadcasted_iota(jnp.int32, (2 * blk, 2 * blk), 1)
    dist = qi + blk - kj
    first_key = jnp.where(mb > 0, 0, blk)
    bias = jnp.where(dist >= 0, jnp.where(dist <= blk, jnp.where(kj >= first_key, 0.0, NEG_INF),
                                          NEG_INF), NEG_INF)
    lane = lax.broadcasted_iota(jnp.int32, (blk, 128), 1)
    lo = lane < HEAD_DIM
    o_parts, l_parts = [], []
    for p in range(GROUP_WIDTH // 128):
        q2 = q[:, p * 128:(p + 1) * 128]
        lhs = jnp.concatenate([jnp.where(lo, q2, 0.0), jnp.where(lo, 0.0, q2)], axis=0).astype(BF16)
        k2 = kcat[:, p * 128:(p + 1) * 128].astype(BF16)
        v2 = vcat[:, p * 128:(p + 1) * 128].astype(BF16)
        s = lax.dot_general(lhs, k2, (((1,), (1,)), ((), ())), preferred_element_type=F32) + bias
        m = jnp.max(s, axis=-1, keepdims=True)
        pexp = jnp.exp(s - m)
        den = jnp.sum(pexp, axis=-1, keepdims=True)
        o2 = jnp.dot(pexp.astype(BF16), v2, preferred_element_type=F32) / den
        lse = jnp.broadcast_to(m + jnp.log(den), (2 * blk, 128))
        o_parts.append(jnp.where(lo, o2[:blk], o2[blk:]))
        l_parts.append(jnp.where(lo, lse[:blk], lse[blk:]))
    o_ref[...] = jnp.concatenate(o_parts, axis=1)
    l_ref[...] = jnp.concatenate(l_parts, axis=1)


def _band_attn(q, k, v, g, dil):
    t = q.shape[0]
    rows = t // dil
    nb = rows // ATT_BLK
    qv, kv_, vv = (a.reshape(rows, dil * ATT_WIDTH) for a in (q, k, v))
    cur = pl.BlockSpec((ATT_BLK, GROUP_WIDTH), lambda r, mb: (mb, r * N_GROUPS + g))
    prev = pl.BlockSpec((ATT_BLK, GROUP_WIDTH), lambda r, mb: (jnp.maximum(mb - 1, 0), r * N_GROUPS + g))
    out = pl.BlockSpec((ATT_BLK, GROUP_WIDTH), lambda r, mb: (mb, r))
    o, lse = pl.pallas_call(
        _band_attn_kernel,
        grid=(dil, nb),
        in_specs=[cur, prev, cur, prev, cur],
        out_specs=[out, out],
        out_shape=[jax.ShapeDtypeStruct((rows, dil * GROUP_WIDTH), F32)] * 2,
        compiler_params=_cparams(2),
        name=f"band_attn_g{g}",
    )(qv, kv_, kv_, vv, vv)
    return o.reshape(t, GROUP_WIDTH), lse.reshape(t, GROUP_WIDTH)


def _cached_attn_kernel(q_ref, kn_ref, vn_ref, c_ref, e_ref, et_ref, o_ref, l_ref, *, dil, seq):
    chunks = 2 * GROUP_WIDTH // 128
    q8 = q_ref[...] * (HEAD_DIM ** -0.5)
    k8 = kn_ref[...]
    v8 = vn_ref[...]
    e_mat = e_ref[...]
    et_mat = et_ref[...]
    m_iota = lax.broadcasted_iota(jnp.int32, (ATT_BLK, 128), 0)
    j_iota = lax.broadcasted_iota(jnp.int32, (seq, 128), 0)
    o_rows, l_rows = [], []
    for t in range(seq):
        r, qq = t % dil, t // dil
        rows = jnp.concatenate(
            [c_ref[0, pl.ds(r * chunks + c, ATT_BLK, stride=dil * chunks), :] for c in range(chunks)],
            axis=1)
        kc, vc = rows[:, :GROUP_WIDTH], rows[:, GROUP_WIDTH:]
        qt = q8[t:t + 1, :]
        sc = jnp.dot((kc * qt).astype(BF16), e_mat, preferred_element_type=F32)
        sc = jnp.where(m_iota >= qq, sc, NEG_INF)
        sn = jnp.dot((k8 * qt).astype(BF16), e_mat, preferred_element_type=F32)
        sn = jnp.where((j_iota & (dil - 1)) == r, jnp.where(j_iota <= t, sn, NEG_INF), NEG_INF)
        mx = jnp.maximum(jnp.max(sc, axis=0, keepdims=True), jnp.max(sn, axis=0, keepdims=True))
        pc = jnp.exp(sc - mx)
        pn = jnp.exp(sn - mx)
        den = jnp.sum(pc, axis=0, keepdims=True) + jnp.sum(pn, axis=0, keepdims=True)
        pce = jnp.dot((pc / den).astype(BF16), et_mat, preferred_element_type=F32)
        pne = jnp.dot((pn / den).astype(BF16), et_mat, preferred_element_type=F32)
        o_rows.append(jnp.sum(pce * vc, axis=0, keepdims=True) + jnp.sum(pne * v8, axis=0, keepdims=True))
        l_rows.append(mx + jnp.log(den))
    o_ref[...] = jnp.concatenate(o_rows, axis=0)
    l_ref[...] = jnp.concatenate(l_rows, axis=0)


def _cached_attn(q, k, v, cache, g, dil, seq, e_mat, et_mat):
    n = q.shape[0]
    nb, wb, width = cache.shape
    cache = cache.reshape(nb, wb * width // 128, 128)
    tok = pl.BlockSpec((seq, GROUP_WIDTH), lambda b: (b, g))
    return pl.pallas_call(
        functools.partial(_cached_attn_kernel, dil=dil, seq=seq),
        grid=(nb,),
        in_specs=[tok, tok, tok, pl.BlockSpec((1, wb * width // 128, 128), lambda b: (b, 0, 0)),
                  _full((GROUP_WIDTH, 128)), _full((128, GROUP_WIDTH))],
        out_specs=[pl.BlockSpec((seq, GROUP_WIDTH), lambda b: (b, 0)),
                   pl.BlockSpec((seq, 128), lambda b: (b, 0))],
        out_shape=[jax.ShapeDtypeStruct((n, GROUP_WIDTH), F32),
                   jax.ShapeDtypeStruct((n, 128), F32)],
        compiler_params=_cparams(1),
        name=f"cached_attn_g{g}",
    )(q, k, v, cache, e_mat, et_mat)


CACHE_COPY_CHUNKS = 8


def _cache_update_kernel(*refs, seq):
    n = N_GROUPS
    olds, news, outs, sem = refs[:n], refs[n:2 * n], refs[2 * n:3 * n], refs[3 * n]
    copies = []
    for g in range(n):
        nb, wb, _ = olds[g].shape
        per = nb // CACHE_COPY_CHUNKS
        for c in range(CACHE_COPY_CHUNKS):
            bs = pl.ds(c * per, per)
            copies.append(pltpu.make_async_copy(
                olds[g].at[bs, pl.ds(seq, wb - seq), :], outs[g].at[bs, pl.ds(0, wb - seq), :],
                sem.at[g, c]))
        copies.append(pltpu.make_async_copy(
            news[g], outs[g].at[:, pl.ds(wb - seq, seq), :], sem.at[g, CACHE_COPY_CHUNKS]))
    for cp in copies:
        cp.start()
    for cp in copies:
        cp.wait()


def _cache_update(olds, news, seq):
    any_spec = pl.BlockSpec(memory_space=pl.ANY)
    return pl.pallas_call(
        functools.partial(_cache_update_kernel, seq=seq),
        in_specs=[any_spec] * (2 * N_GROUPS),
        out_specs=[any_spec] * N_GROUPS,
        out_shape=[jax.ShapeDtypeStruct(o.shape, o.dtype) for o in olds],
        scratch_shapes=[pltpu.SemaphoreType.DMA((N_GROUPS, CACHE_COPY_CHUNKS + 1))],
        name="cache_update",
    )(*olds, *news)


def _merge_kernel(y_ref, o1_ref, o2_ref, o3_ref, l1_ref, l2_ref, l3_ref, gl_ref, ga_ref, x_ref,
                  wl_ref, wa_ref, wo_ref, out_ref):
    l1, l2, l3 = l1_ref[...], l2_ref[...], l3_ref[...]
    mx = jnp.maximum(jnp.maximum(l1, l2), l3)
    e1, e2, e3 = jnp.exp(l1 - mx), jnp.exp(l2 - mx), jnp.exp(l3 - mx)
    tot = e1 + e2 + e3
    o = (e1 / tot) * o1_ref[...] + (e2 / tot) * o2_ref[...] + (e3 / tot) * o3_ref[...]
    lru_proj = jnp.dot(y_ref[...], wl_ref[...], preferred_element_type=F32)
    att_proj = jnp.dot(o.astype(BF16), wa_ref[...], preferred_element_type=F32)
    merged = jax.nn.sigmoid(gl_ref[...]) * lru_proj + jax.nn.sigmoid(ga_ref[...]) * att_proj
    out_ref[...] = x_ref[...] + jnp.dot(merged.astype(BF16), wo_ref[...], preferred_element_type=F32)


def _merge(y, os_, ls_, gl, ga, x, wl, wa, wo, tm=512):
    n = x.shape[0]
    row = lambda w: pl.BlockSpec((tm, w), lambda i: (i, 0))
    return pl.pallas_call(
        _merge_kernel,
        grid=(n // tm,),
        in_specs=[row(D_LRU_PAD)] + [row(GROUP_WIDTH)] * 6 + [row(D_MODEL)] * 3
        + [_full((D_LRU_PAD, D_MODEL)), _full((GROUP_WIDTH, D_MODEL)), _full((D_MODEL, D_MODEL))],
        out_specs=row(D_MODEL),
        out_shape=jax.ShapeDtypeStruct((n, D_MODEL), F32),
        compiler_params=_cparams(1),
        name="merge",
    )(y, *os_, *ls_, gl, ga, x, wl, wa, wo)


PEER_TN = 512
PEER_TE = 1024


def _candidate_blocks():
    k = PEER_TOPK
    blocks = [("col", 0, 0, 16, 16)]
    for b in (1, 2, 3):
        blocks.append(("col", b, 0, 8, k // (b + 1)))
    blocks.append(("row", 0, 8, 16, 8))
    for a in (0, 1):
        blocks.append(("row4", a, 4, 8, 4))
    blocks.append(("one", 2, 4, 0, 0))
    return blocks


def _check_candidate_blocks():
    seen = []
    for kind, p, lo, hi, cnt in _candidate_blocks():
        if kind == "col":
            seen += [(a, p) for a in range(cnt)]
        elif kind == "row":
            seen += [(p, b) for b in range(lo, hi)]
        elif kind == "row4":
            seen += [(p, b) for b in range(lo, hi)]
        else:
            seen.append((p, lo))
    want = sorted((a, b) for a in range(PEER_TOPK) for b in range(PEER_TOPK)
                  if (a + 1) * (b + 1) <= PEER_TOPK)
    assert sorted(seen) == want, (sorted(seen), want)


_check_candidate_blocks()


def _top_values(x, n):
    rows, lanes = x.shape
    iota = lax.broadcasted_iota(jnp.int32, (rows, lanes), 0).astype(F32)
    slot = lax.broadcasted_iota(jnp.int32, (n, lanes), 0)

    def body(i, carry):
        cur, vals = carry
        m = jnp.max(cur, axis=0, keepdims=True)
        first = jnp.min(jnp.where(cur == m, iota, float(rows)), axis=0, keepdims=True)
        cur = jnp.where(iota == first, NEG_BIG, cur)
        vals = jnp.where(slot == i, m, vals)
        return cur, vals

    _, vals = lax.fori_loop(0, n, body, (x, jnp.full((n, lanes), NEG_BIG, F32)))
    return vals


def _threshold_and_norm(v1, v2):
    lanes = v1.shape[1]
    i8 = lax.broadcasted_iota(jnp.int32, (8, lanes), 0)
    parts = []
    for kind, p, lo, hi, cnt in _candidate_blocks():
        if kind == "col":
            blk = v1[0:hi, :] + v2[p:p + 1, :]
            if cnt < hi:
                blk = jnp.where(i8 < cnt, blk, NEG_BIG)
        elif kind == "row":
            blk = v2[lo:hi, :] + v1[p:p + 1, :]
        elif kind == "row4":
            blk = jnp.where(i8 >= 4, v2[0:8, :] + v1[p:p + 1, :], NEG_BIG)
        else:
            blk = jnp.where(i8 == 0, v1[p:p + 1, :] + v2[lo:lo + 1, :], NEG_BIG)
        parts.append(blk)
    cand = jnp.concatenate(parts, axis=0)
    best = _top_values(cand, PEER_TOPK)
    den = jnp.sum(jnp.exp(best - best[0:1, :]), axis=0, keepdims=True)
    return best[PEER_TOPK - 1:PEER_TOPK, :], den


def _peer_kernel(x_ref, gffn_ref, gfin_ref, wq_ref, k1_ref, k2_ref, u_ref, vt_ref, y_ref,
                 hnt_s, s1r_s, ar_s, s2_s, e2_s, tau_s, acc_s):
    e = pl.program_id(1)
    n_e = pl.num_programs(1)
    tn = PEER_TN
    nk = PEER_N_KEYS

    @pl.when(e == 0)
    def _prep():
        x = x_ref[...]
        var = jnp.mean(x * x, axis=-1, keepdims=True)
        hn = x * lax.rsqrt(var + NORM_EPS) * gffn_ref[...]
        hnt = hn.T.astype(BF16)
        hnt_s[...] = hnt
        acc_s[...] = jnp.zeros_like(acc_s)
        for h in range(PEER_HEADS):
            qt = jnp.dot(wq_ref[h * PEER_D_KEY:(h + 1) * PEER_D_KEY, :], hnt,
                         preferred_element_type=F32)
            s1 = jnp.dot(k1_ref[h], qt[:nk].astype(BF16), preferred_element_type=F32)
            s2 = jnp.dot(k2_ref[h], qt[nk:].astype(BF16), preferred_element_type=F32)
            v1 = _top_values(s1, PEER_TOPK)
            v2 = _top_values(s2, PEER_TOPK)
            tau, den = _threshold_and_norm(v1, v2)
            a1 = jnp.exp(s1 - v1[0:1, :]) / den
            for c in range(tn // 128):
                lanes = slice(c * 128, (c + 1) * 128)
                s1r_s[c, pl.ds(h, nk, stride=PEER_HEADS), :] = s1[:, lanes]
                ar_s[c, pl.ds(h, nk, stride=PEER_HEADS), :] = a1[:, lanes]
            s2_s[h] = s2
            e2_s[h] = jnp.exp(s2 - v2[0:1, :])
            tau_s[h:h + 1, :] = tau

    act = jnp.dot(u_ref[...], hnt_s[...], preferred_element_type=F32)
    gact = jax.nn.gelu(act)
    p_rows = []
    for ii in range(PEER_TE // nk):
        base = pl.multiple_of((e * (PEER_TE // nk) + ii) * PEER_HEADS, PEER_HEADS)
        s1_rows = jnp.concatenate([s1r_s[c, pl.ds(base, PEER_HEADS), :] for c in range(tn // 128)], axis=1)
        a_rows = jnp.concatenate([ar_s[c, pl.ds(base, PEER_HEADS), :] for c in range(tn // 128)], axis=1)
        wt = jnp.zeros((nk, tn), F32)
        for h in range(PEER_HEADS):
            pair = s2_s[h] + s1_rows[h:h + 1, :]
            wt = wt + jnp.where(pair >= tau_s[h:h + 1, :], e2_s[h], 0.0) * a_rows[h:h + 1, :]
        p_rows.append((wt * gact[ii * nk:(ii + 1) * nk, :]).astype(BF16))
    pmat = jnp.concatenate(p_rows, axis=0)
    acc_s[...] += jnp.dot(vt_ref[...], pmat, preferred_element_type=F32)

    @pl.when(e == n_e - 1)
    def _finish():
        x2 = x_ref[...] + acc_s[...].T
        var = jnp.mean(x2 * x2, axis=-1, keepdims=True)
        y_ref[...] = x2 * lax.rsqrt(var + NORM_EPS) * gfin_ref[...]


def _peer(x1, gffn, gfin, wq_t, k1, k2, u, vt):
    n = x1.shape[0]
    n_exp = u.shape[0]
    tn, te = PEER_TN, PEER_TE
    return pl.pallas_call(
        _peer_kernel,
        grid=(n // tn, n_exp // te),
        in_specs=[pl.BlockSpec((tn, D_MODEL), lambda i, e: (i, 0)),
                  _full((1, D_MODEL)), _full((1, D_MODEL)),
                  _full((PEER_HEADS * PEER_D_KEY, D_MODEL)),
                  _full((PEER_HEADS, PEER_N_KEYS, PEER_D_KEY // 2)),
                  _full((PEER_HEADS, PEER_N_KEYS, PEER_D_KEY // 2)),
                  pl.BlockSpec((te, D_MODEL), lambda i, e: (e, 0)),
                  pl.BlockSpec((D_MODEL, te), lambda i, e: (0, e))],
        out_specs=pl.BlockSpec((tn, D_MODEL), lambda i, e: (i, 0)),
        out_shape=jax.ShapeDtypeStruct((n, D_MODEL), F32),
        scratch_shapes=[pltpu.VMEM((D_MODEL, tn), BF16),
                        pltpu.VMEM((tn // 128, PEER_N_KEYS * PEER_HEADS, 128), F32),
                        pltpu.VMEM((tn // 128, PEER_N_KEYS * PEER_HEADS, 128), F32),
                        pltpu.VMEM((PEER_HEADS, PEER_N_KEYS, tn), F32),
                        pltpu.VMEM((PEER_HEADS, PEER_N_KEYS, tn), F32),
                        pltpu.VMEM((PEER_HEADS, tn), F32),
                        pltpu.VMEM((D_MODEL, tn), F32)],
        compiler_params=_cparams(2),
        name="peer",
    )(x1, gffn, gfin, wq_t, k1, k2, u, vt)


def _rope_tables(pos):
    half = HEAD_DIM // 2
    inv = ROPE_THETA ** (-jnp.arange(half, dtype=F32) / half)
    ang = pos.astype(F32)[:, None] * inv[None, :]
    cos, sin = jnp.cos(ang), jnp.sin(ang)
    return (jnp.concatenate([cos, cos, cos, cos], axis=1),
            jnp.concatenate([-sin, sin, -sin, sin], axis=1))


def _pad_lru(a, axis):
    pad = [(0, 0)] * a.ndim
    pad[axis] = (0, D_LRU_PAD - D_LRU)
    return jnp.pad(a, pad)


def _block_diag(w):
    nblk, bw, _ = w.shape
    eye = jnp.eye(nblk, dtype=w.dtype)
    dense = (eye[:, None, :, None] * w[:, :, None, :]).reshape(nblk * bw, nblk * bw)
    return jnp.pad(dense, ((0, D_LRU_PAD - D_LRU), (0, D_LRU_PAD - D_LRU)))


def _head_sum_matrices():
    lane_head = np.arange(GROUP_WIDTH) // HEAD_DIM
    e = (lane_head[:, None] == np.arange(128)[None, :]).astype(np.float32)
    return jnp.asarray(e, BF16), jnp.asarray(e.T, BF16)


def _trunk_front(x, cos_t, sin_t, wts):
    xl, yl, gl, ga = _inproj(x, wts["norm_mix"], wts["w_a"],
                             (D_LRU_PAD, D_LRU_PAD, D_MODEL, D_MODEL), (False,) * 4)
    q, k, v = _inproj(x, wts["norm_mix"], wts["w_b"], (ATT_WIDTH,) * 3, (True, True, False),
                      cos_t, sin_t)
    return xl, yl, gl, ga, q, k, v


def _trunk_back(x, y, os_, ls_, gl, ga, wts):
    x1 = _merge(y, os_, ls_, gl, ga, x, wts["w_lru_out"], wts["w_att_out"], wts["w_out"])
    return _peer(x1, wts["norm_ffn"], wts["norm_final"], wts["wq_t"], wts["k1"], wts["k2"],
                 wts["peer_u"], wts["peer_vt"])


def kernel(x_prompt, x_sample, cache_kv_g1, cache_kv_g2, cache_kv_g3, state_lru_h, state_conv,
           norm_mix, w_in, conv_w, conv_b, w_rg, b_rg, w_ig, b_ig, lru_lambda, w_lru_out,
           w_att_out, w_out, norm_ffn, w_peer_q, peer_sub_keys, peer_u, peer_v, norm_final):
    _, t_len, _ = x_prompt.shape
    db, ds, _ = x_sample.shape
    past_len = 8192
    caches = (cache_kv_g1, cache_kv_g2, cache_kv_g3)
    for c, (window, dil) in zip(caches, DILATED_GROUPS):
        assert c.shape[2] == window and window // dil == ATT_BLK
    assert ds == 8 and w_in.shape[0] == 1

    cuts = np.cumsum([D_LRU, D_LRU, ATT_WIDTH, ATT_WIDTH, ATT_WIDTH, D_MODEL])
    w_xl, w_yl, w_q, w_k, w_v, w_gl, w_ga = jnp.split(w_in[0], cuts, axis=1)
    wts = {
        "norm_mix": norm_mix[0][None, :],
        "w_a": jnp.concatenate([_pad_lru(w_xl, 1), _pad_lru(w_yl, 1), w_gl, w_ga], axis=1).astype(BF16),
        "w_b": jnp.concatenate([w_q, w_k, w_v], axis=1).astype(BF16),
        "w_lru_out": _pad_lru(w_lru_out[0], 0).astype(BF16),
        "w_att_out": w_att_out[0].astype(BF16),
        "w_out": w_out[0].astype(BF16),
        "norm_ffn": norm_ffn[0][None, :],
        "norm_final": norm_final[None, :],
        "wq_t": w_peer_q[0].reshape(D_MODEL, PEER_HEADS * PEER_D_KEY).T.astype(BF16),
        "k1": peer_sub_keys[0, 0].astype(BF16),
        "k2": peer_sub_keys[0, 1].astype(BF16),
        "peer_u": peer_u[0].astype(BF16),
        "peer_vt": peer_v[0].T.astype(BF16),
    }
    lw = (jnp.pad(_pad_lru(conv_w[0], 1), ((0, 8 - CONV_W), (0, 0))),
          _pad_lru(conv_b[0], 0)[None, :],
          jnp.concatenate([_block_diag(w_rg[0]), _block_diag(w_ig[0])], axis=1).astype(BF16),
          jnp.concatenate([_pad_lru(b_rg[0], 0), _pad_lru(b_ig[0], 0)])[None, :],
          _pad_lru(lru_lambda[0], 0)[None, :])
    e_mat, et_mat = _head_sum_matrices()

    xp = x_prompt.reshape(t_len, D_MODEL)
    cos_p, sin_p = _rope_tables(jnp.arange(t_len, dtype=jnp.int32))
    xl, yl, gl, ga, q, k, v = _trunk_front(xp, cos_p, sin_p, wts)
    y_lru, h_last = _lru_prompt(xl, yl, lw)
    os_, ls_ = [], []
    for g, (_, dil) in enumerate(DILATED_GROUPS):
        o, lse = _band_attn(q, k, v, g, dil)
        os_.append(o)
        ls_.append(lse)
    y_prompt = _trunk_back(xp, y_lru, os_, ls_, gl, ga, wts).reshape(1, t_len, D_MODEL)
    kv_prompt = []
    for g, (window, _) in enumerate(DILATED_GROUPS):
        wb = min(window, t_len)
        cols = slice(g * GROUP_WIDTH, (g + 1) * GROUP_WIDTH)
        kv_prompt.append(jnp.stack([k[t_len - wb:, cols], v[t_len - wb:, cols]], axis=1)
                         .reshape(1, 1, wb, 2, HEADS, HEAD_DIM))
    h_prompt = h_last[0:1, :D_LRU].reshape(1, 1, D_LRU)
    conv_prompt = xl[t_len - (CONV_W - 1):, :D_LRU].reshape(1, 1, CONV_W - 1, D_LRU)

    n_s = db * ds
    xs = x_sample.reshape(n_s, D_MODEL)
    cos_s, sin_s = _rope_tables(past_len + (jnp.arange(n_s, dtype=jnp.int32) % ds))
    xl_s, yl_s, gl_s, ga_s, q_s, k_s, v_s = _trunk_front(xs, cos_s, sin_s, wts)
    st = jnp.pad(_pad_lru(state_conv[0], 2), ((0, 0), (ds - (CONV_W - 1), 0), (0, 0)))
    h0 = jnp.repeat(_pad_lru(state_lru_h[0], 1), ds, axis=0)
    y_lru_s, hs_s = _lru_sample(xl_s, yl_s, st.reshape(n_s, D_LRU_PAD), h0, lw, ds)
    os_s, ls_s, olds, news = [], [], [], []
    for g, (window, dil) in enumerate(DILATED_GROUPS):
        cache = caches[g][0].reshape(db, window, 2 * GROUP_WIDTH)
        o, lse = _cached_attn(q_s, k_s, v_s, cache, g, dil, ds, e_mat, et_mat)
        os_s.append(o)
        ls_s.append(jnp.repeat(lse[:, :HEADS], HEAD_DIM, axis=1))
        cols = slice(g * GROUP_WIDTH, (g + 1) * GROUP_WIDTH)
        olds.append(cache)
        news.append(jnp.concatenate([k_s[:, cols], v_s[:, cols]], axis=1).reshape(db, ds, 2 * GROUP_WIDTH))
    kv_sample = [c.reshape(1, db, c.shape[1], 2, HEADS, HEAD_DIM)
                 for c in _cache_update(olds, news, ds)]
    y_sample = _trunk_back(xs, y_lru_s, os_s, ls_s, gl_s, ga_s, wts).reshape(db, ds, D_MODEL)
    h_sample = hs_s.reshape(db, ds, D_LRU_PAD)[:, ds - 1, :D_LRU].reshape(1, db, D_LRU)
    conv_sample = (xl_s.reshape(db, ds, D_LRU_PAD)[:, ds - (CONV_W - 1):, :D_LRU]
                   .reshape(1, db, CONV_W - 1, D_LRU))

    return (y_prompt, y_sample, kv_prompt[0], kv_prompt[1], kv_prompt[2], h_prompt, conv_prompt,
            kv_sample[0], kv_sample[1], kv_sample[2], h_sample, conv_sample)
```
